```python
import math
import jax, jax.numpy as jnp
from jax import lax
import numpy as np

D_MODEL = 2048
BATCH = 8
SEQ = 2048
DEPTH = 1

GRID_W = 64
CTX_LEN = 256
NA_HEADS = 8
NA_HEAD_DIM = 128
NA_WIDTH = NA_HEADS * NA_HEAD_DIM
NA_WIN_ROWS = 8
NA_WIN_COLS = 16
DIFF_HEADS = 8
DIFF_HEAD_DIM = 64
DIFF_V_DIM = 2 * DIFF_HEAD_DIM
DIFF_QK_WIDTH = DIFF_HEADS * 2 * DIFF_HEAD_DIM
DIFF_V_WIDTH = DIFF_HEADS * DIFF_V_DIM
N_QKV = 3 * NA_WIDTH + 2 * DIFF_QK_WIDTH + DIFF_V_WIDTH
N_GATE = 2 * D_MODEL
D_FF = 5632
CONV_WIDTH = 3
Q_BLOCK = 128
ROPE_BASE = 10000.0
EPS = 1e-6

kernel_name = "hybrid_natten_diffattn_convffn_dit_block"


def rms_norm(x, g):
    xf = x.astype(jnp.float32)
    y = xf * lax.rsqrt(jnp.mean(xf * xf, axis=-1, keepdims=True) + EPS)
    return (y * g.astype(jnp.float32)).astype(x.dtype)


def modulate(h, shift, scale):
    return h * (1 + scale) + shift


def axial_rope_tables(n_tokens, head_dim):
    t = jnp.arange(n_tokens)
    row = (t // GRID_W).astype(jnp.float32)
    col = (t % GRID_W).astype(jnp.float32)
    n_freq = head_dim // 4
    inv = ROPE_BASE ** (-jnp.arange(n_freq, dtype=jnp.float32) / n_freq)
    ang_r = row[:, None] * inv[None, :]
    ang_c = col[:, None] * inv[None, :]
    ang = jnp.concatenate([ang_r, ang_r, ang_c, ang_c], axis=-1)
    return jnp.cos(ang), jnp.sin(ang)


def apply_rope(x, cos, sin):
    n, d = cos.shape
    shp = (1, n) + (1,) * (x.ndim - 3) + (d,)
    cos = cos.reshape(shp).astype(x.dtype)
    sin = sin.reshape(shp).astype(x.dtype)
    q = d // 4
    x1, x2, x3, x4 = x[..., :q], x[..., q:2 * q], x[..., 2 * q:3 * q], x[..., 3 * q:]
    rot = jnp.concatenate([-x2, x1, -x4, x3], axis=-1)
    return x * cos + rot * sin


def split_projection(p):
    b, n = p.shape[:2]
    cuts = [NA_WIDTH, 2 * NA_WIDTH, 3 * NA_WIDTH,
            3 * NA_WIDTH + DIFF_QK_WIDTH, 3 * NA_WIDTH + 2 * DIFF_QK_WIDTH]
    qa, ka, va, qb, kb, vb = jnp.split(p, cuts, axis=-1)
    qa = qa.reshape(b, n, NA_HEADS, NA_HEAD_DIM)
    ka = ka.reshape(b, n, NA_HEADS, NA_HEAD_DIM)
    va = va.reshape(b, n, NA_HEADS, NA_HEAD_DIM)
    qb = qb.reshape(b, n, DIFF_HEADS, 2, DIFF_HEAD_DIM)
    kb = kb.reshape(b, n, DIFF_HEADS, 2, DIFF_HEAD_DIM)
    vb = vb.reshape(b, n, DIFF_HEADS, DIFF_V_DIM)
    return qa, ka, va, qb, kb, vb


def dense_attention(q, k, v):
    s = jnp.einsum('bqhd,bkhd->bhqk', q, k).astype(jnp.float32) * (q.shape[-1] ** -0.5)
    p = jax.nn.softmax(s, axis=-1).astype(v.dtype)
    o = jnp.einsum('bhqk,bkhd->bqhd', p, v)
    return o.reshape(o.shape[0], o.shape[1], -1)


def neighbourhood_attention_latent(q, k, v, k_ctx, v_ctx, rpb):
    b, n, h, d = q.shape
    rows = n // GRID_W
    wr = min(NA_WIN_ROWS, rows)
    wc = NA_WIN_COLS
    n_win = wr * wc
    scale = d ** -0.5
    row_start = jnp.clip(jnp.arange(rows) - wr // 2, 0, rows - wr)
    cols = jnp.arange(GRID_W)
    key_cols = jnp.clip(cols - wc // 2, 0, GRID_W - wc)[:, None] + jnp.arange(wc)[None, :]
    col_bias_idx = key_cols - cols[:, None] + NA_WIN_COLS - 1
    q_rows = q.reshape(b, rows, GRID_W, h, d).swapaxes(0, 1)

    def one_row(args):
        r, qr = args
        key_rows = row_start[r] + jnp.arange(wr)
        idx = (key_rows[None, :, None] * GRID_W + key_cols[:, None, :]).reshape(GRID_W, n_win)
        kw = jnp.take(k, idx, axis=1)
        vw = jnp.take(v, idx, axis=1)
        row_bias_idx = key_rows - r + NA_WIN_ROWS - 1
        bias = rpb[:, row_bias_idx[None, :, None], col_bias_idx[:, None, :]]
        bias = bias.reshape(h, GRID_W, n_win).astype(jnp.float32)
        s_win = jnp.einsum('bqhd,bqkhd->bhqk', qr, kw).astype(jnp.float32) * scale + bias[None]
        s_ctx = jnp.einsum('bqhd,bchd->bhqc', qr, k_ctx).astype(jnp.float32) * scale
        p = jax.nn.softmax(jnp.concatenate([s_win, s_ctx], axis=-1), axis=-1).astype(v.dtype)
        return (jnp.einsum('bhqk,bqkhd->bqhd', p[..., :n_win], vw)
                + jnp.einsum('bhqc,bchd->bqhd', p[..., n_win:], v_ctx))

    out = lax.map(one_row, (jnp.arange(rows), q_rows))
    return out.swapaxes(0, 1).reshape(b, n, h * d)


def diff_attn_block(q, k, v, lam):
    s = jnp.einsum('bqhmd,bkhmd->bhmqk', q, k).astype(jnp.float32) * (DIFF_HEAD_DIM ** -0.5)
    p = jax.nn.softmax(s, axis=-1)
    p = (p[:, :, 0] - lam * p[:, :, 1]).astype(v.dtype)
    return jnp.einsum('bhqk,bkhd->bqhd', p, v)


def diff_attention_latent(q, k, v, k_ctx, v_ctx, lam):
    b, n = q.shape[:2]
    k_all = jnp.concatenate([k, k_ctx], axis=1)
    v_all = jnp.concatenate([v, v_ctx], axis=1)
    q_blocks = q.reshape(b, n // Q_BLOCK, Q_BLOCK, DIFF_HEADS, 2, DIFF_HEAD_DIM).swapaxes(0, 1)
    out = lax.map(lambda qb: diff_attn_block(qb, k_all, v_all, lam), q_blocks)
    return out.swapaxes(0, 1).reshape(b, n, DIFF_HEADS, DIFF_V_DIM)


def diff_head_out(o, g, lambda_init):
    o = rms_norm(o, g) * (1 - lambda_init)
    return o.reshape(o.shape[0], o.shape[1], DIFF_V_WIDTH)


def merge_branches(a, bb, gate_logits, w_pa, w_pb, w_out):
    ga, gb = jnp.split(gate_logits, 2, axis=-1)
    y = jax.nn.sigmoid(ga) * (a @ w_pa) + jax.nn.sigmoid(gb) * (bb @ w_pb)
    return y @ w_out


def conv_ffn(h, w_up, conv_w, conv_b, w_down):
    u = h @ w_up
    ch = u.shape[-1]
    u = lax.conv_general_dilated(
        u, conv_w[:, None, :].astype(u.dtype), window_strides=(1,),
        padding=((CONV_WIDTH // 2, CONV_WIDTH // 2),),
        dimension_numbers=('NWC', 'WIO', 'NWC'), feature_group_count=ch) + conv_b
    gate, up = jnp.split(u, 2, axis=-1)
    return (jax.nn.silu(gate) * up) @ w_down


def setup_inputs(seed: int = 0) -> dict:
    key = jax.random.key(seed)
    ks = jax.random.split(key, 24)
    f = jnp.float32
    nrm = lambda k, shp, s: jax.random.normal(k, shp, f) * s
    return {
        "x": nrm(ks[0], (BATCH, SEQ, D_MODEL), 1.0),
        "c": nrm(ks[1], (BATCH, D_MODEL), 1.0),
        "ctx": nrm(ks[2], (BATCH, CTX_LEN, D_MODEL), 1.0),
        "c_ctx": nrm(ks[3], (D_MODEL,), 1.0),
        "ada_w": nrm(ks[4], (DEPTH, D_MODEL, 6 * D_MODEL), 0.5 * D_MODEL ** -0.5),
        "ada_b": nrm(ks[5], (DEPTH, 6 * D_MODEL), 0.02),
        "norm1_g": 1.0 + nrm(ks[6], (DEPTH, D_MODEL), 0.05),
        "w_in": nrm(ks[7], (DEPTH, D_MODEL, N_QKV + N_GATE), D_MODEL ** -0.5),
        "na_rpb": nrm(ks[8], (DEPTH, NA_HEADS, 2 * NA_WIN_ROWS - 1, 2 * NA_WIN_COLS - 1), 0.1),
        "diff_lambda": nrm(ks[9], (DEPTH, 4, DIFF_HEAD_DIM), 0.1),
        "diff_norm_g": 1.0 + nrm(ks[10], (DEPTH, DIFF_V_DIM), 0.05),
        "w_branch_a": nrm(ks[11], (DEPTH, NA_WIDTH, D_MODEL), NA_WIDTH ** -0.5),
        "w_branch_b": nrm(ks[12], (DEPTH, DIFF_V_WIDTH, D_MODEL), DIFF_V_WIDTH ** -0.5),
        "w_out": nrm(ks[13], (DEPTH, D_MODEL, D_MODEL), D_MODEL ** -0.5),
        "norm2_g": 1.0 + nrm(ks[14], (DEPTH, D_MODEL), 0.05),
        "ffn_w_up": nrm(ks[15], (DEPTH, D_MODEL, 2 * D_FF), D_MODEL ** -0.5),
        "ffn_conv_w": nrm(ks[16], (DEPTH, CONV_WIDTH, 2 * D_FF), CONV_WIDTH ** -0.5),
        "ffn_conv_b": nrm(ks[17], (DEPTH, 2 * D_FF), 0.02),
        "ffn_w_down": nrm(ks[18], (DEPTH, D_FF, D_MODEL), D_FF ** -0.5),
        "final_norm_g": 1.0 + nrm(ks[19], (D_MODEL,), 0.05),
    }


def reference(x, c, ctx, c_ctx, ada_w, ada_b, norm1_g, w_in, na_rpb, diff_lambda,
              diff_norm_g, w_branch_a, w_branch_b, w_out, norm2_g, ffn_w_up,
              ffn_conv_w, ffn_conv_b, ffn_w_down, final_norm_g):
    n_lat = x.shape[1]
    cos, sin = axial_rope_tables(n_lat, DIFF_HEAD_DIM)
    for i in range(DEPTH):
        last = i == DEPTH - 1
        lambda_init = 0.8 - 0.6 * math.exp(-0.3 * i)
        mod = jax.nn.silu(c) @ ada_w[i] + ada_b[i]
        sh1, sc1, g1, sh2, sc2, g2 = jnp.split(mod[:, None, :], 6, axis=-1)
        mod_c = jax.nn.silu(c_ctx) @ ada_w[i] + ada_b[i]
        csh1, csc1, cg1, csh2, csc2, cg2 = jnp.split(mod_c, 6, axis=-1)

        h = modulate(rms_norm(x, norm1_g[i]), sh1, sc1)
        hc = modulate(rms_norm(ctx, norm1_g[i]), csh1, csc1)
        proj = h @ w_in[i]
        qa, ka, va, qb, kb, vb = split_projection(proj[..., :N_QKV])
        gates = proj[..., N_QKV:]
        qa_c, ka_c, va_c, qb_c, kb_c, vb_c = split_projection(hc @ w_in[i][:, :N_QKV])

        lq1, lk1, lq2, lk2 = [diff_lambda[i, j].astype(jnp.float32) for j in range(4)]
        lam = jnp.exp(jnp.sum(lq1 * lk1)) - jnp.exp(jnp.sum(lq2 * lk2)) + lambda_init

        a_lat = neighbourhood_attention_latent(qa, ka, va, ka_c, va_c, na_rpb[i])
        b_lat = diff_attention_latent(apply_rope(qb, cos, sin), apply_rope(kb, cos, sin),
                                      vb, kb_c, vb_c, lam)
        b_lat = diff_head_out(b_lat, diff_norm_g[i], lambda_init)
        x = x + g1 * merge_branches(a_lat, b_lat, gates, w_branch_a[i], w_branch_b[i], w_out[i])

        h2 = modulate(rms_norm(x, norm2_g[i]), sh2, sc2)
        x = x + g2 * conv_ffn(h2, ffn_w_up[i], ffn_conv_w[i], ffn_conv_b[i], ffn_w_down[i])

        if not last:
            gates_c = hc @ w_in[i][:, N_QKV:]
            a_c = dense_attention(qa_c, ka_c, va_c)
            b_c = diff_head_out(diff_attn_block(qb_c, kb_c, vb_c, lam), diff_norm_g[i], lambda_init)
            ctx = ctx + cg1 * merge_branches(a_c, b_c, gates_c, w_branch_a[i], w_branch_b[i], w_out[i])
            hc2 = modulate(rms_norm(ctx, norm2_g[i]), csh2, csc2)
            ctx = ctx + cg2 * conv_ffn(hc2, ffn_w_up[i], ffn_conv_w[i], ffn_conv_b[i], ffn_w_down[i])
    return rms_norm(x, final_norm_g)
```

```python
import functools
import math

import numpy as np
import jax
import jax.numpy as jnp
from jax import lax
from jax.experimental import pallas as pl
from jax.experimental.pallas import tpu as pltpu

D_MODEL = 2048
GRID_W = 64
NA_HEADS = 8
NA_HEAD_DIM = 128
NA_WIDTH = NA_HEADS * NA_HEAD_DIM
NA_WIN_ROWS = 8
NA_WIN_COLS = 16
DIFF_HEADS = 8
DIFF_HEAD_DIM = 64
DIFF_V_DIM = 2 * DIFF_HEAD_DIM
DIFF_QK_WIDTH = DIFF_HEADS * 2 * DIFF_HEAD_DIM
DIFF_V_WIDTH = DIFF_HEADS * DIFF_V_DIM
N_QKV = 3 * NA_WIDTH + 2 * DIFF_QK_WIDTH + DIFF_V_WIDTH
N_GATE = 2 * D_MODEL
D_FF = 5632
ROPE_BASE = 10000.0
EPS = 1e-6
LAMBDA_INIT = 0.8 - 0.6 * math.exp(-0.3 * 0)

LANES = 128
VMEM_LIMIT = 56 * 1024 * 1024
MASK_VALUE = -1e30

NA_Q_ROWS = 4
NA_BAND_ROWS = NA_Q_ROWS + NA_WIN_ROWS - 1

F32 = jnp.float32
BF16 = jnp.bfloat16


def _params(semantics):
    return pltpu.CompilerParams(dimension_semantics=semantics, vmem_limit_bytes=VMEM_LIMIT)


def _dot(a, b):
    return jnp.dot(a, b, preferred_element_type=F32)


def _dot_nt(a, b):
    return lax.dot_general(a, b, (((1,), (1,)), ((), ())), preferred_element_type=F32)


def _rms(x, g):
    return x * lax.rsqrt(jnp.mean(x * x, axis=-1, keepdims=True) + EPS) * g


def _adaln_kernel(c_ref, w_ref, b_ref, o_ref):
    c = c_ref[...]
    s = (c * jax.nn.sigmoid(c)).astype(BF16)
    o_ref[...] = _dot(s, w_ref[...].astype(BF16)) + b_ref[...]


def _adaln(cc, ada_w, ada_b):
    rows, d = cc.shape
    n = ada_w.shape[1]
    tn = 512
    return pl.pallas_call(
        _adaln_kernel,
        grid=(n // tn,),
        in_specs=[
            pl.BlockSpec((rows, d), lambda j: (0, 0)),
            pl.BlockSpec((d, tn), lambda j: (0, j)),
            pl.BlockSpec((1, tn), lambda j: (0, j)),
        ],
        out_specs=pl.BlockSpec((rows, tn), lambda j: (0, j)),
        out_shape=jax.ShapeDtypeStruct((rows, n), F32),
        compiler_params=_params(("arbitrary",)),
        name="adaln",
    )(cc, ada_w, ada_b.reshape(1, n))


def _in_proj_kernel(x_ref, g_ref, sh_ref, sc_ref, w_ref, cos_ref, sa_ref, sb_ref, o_ref, h_ref,
                    *, rope_tiles, q_tile):
    j = pl.program_id(1)

    @pl.when(j == 0)
    def _():
        h = _rms(x_ref[...], g_ref[...]) * (1.0 + sc_ref[0]) + sh_ref[0]
        h_ref[...] = h.astype(BF16)

    acc = _dot(h_ref[...], w_ref[...])
    is_rope = functools.reduce(jnp.logical_or, [j == t for t in rope_tiles]) if rope_tiles else None

    def plain():
        o_ref[...] = acc.astype(o_ref.dtype)

    def rope():
        qscale = jnp.where(j == q_tile, DIFF_HEAD_DIM ** -0.5, 1.0).astype(F32)
        cos, sa, sb = cos_ref[...], sa_ref[...], sb_ref[...]
        n = acc.shape[1]
        for hd in range(n // LANES):
            a = acc[:, hd * LANES:(hd + 1) * LANES]
            r = a * cos + pltpu.roll(a, LANES - 16, 1) * sa + pltpu.roll(a, 16, 1) * sb
            o_ref[:, hd * LANES:(hd + 1) * LANES] = (r * qscale).astype(o_ref.dtype)

    if rope_tiles:
        pl.when(is_rope)(rope)
        pl.when(jnp.logical_not(is_rope))(plain)
    else:
        plain()


def _in_proj(x2, g, sh, sc, w, tables, *, rows_per_mod, col_tiles, rope_tiles, q_tile, tm, tn):
    m, d = x2.shape
    cos, sa, sb = tables
    seq = cos.shape[0]
    n_j = len(col_tiles)
    col_tiles = tuple(col_tiles)

    def w_map(i, j):
        col = j
        if col_tiles != tuple(range(n_j)):
            col = sum(jnp.where(j == k, t, 0) for k, t in enumerate(col_tiles))
        return (0, col)

    mod_map = lambda i, j: ((i * tm) // rows_per_mod, 0, 0)
    tab_map = lambda i, j: ((i * tm % seq) // tm, 0)
    kern = functools.partial(_in_proj_kernel, rope_tiles=tuple(rope_tiles), q_tile=q_tile)
    return pl.pallas_call(
        kern,
        grid=(m // tm, n_j),
        in_specs=[
            pl.BlockSpec((tm, d), lambda i, j: (i, 0)),
            pl.BlockSpec((1, d), lambda i, j: (0, 0)),
            pl.BlockSpec((1, 1, d), mod_map),
            pl.BlockSpec((1, 1, d), mod_map),
            pl.BlockSpec((d, tn), w_map),
            pl.BlockSpec((tm, LANES), tab_map),
            pl.BlockSpec((tm, LANES), tab_map),
            pl.BlockSpec((tm, LANES), tab_map),
        ],
        out_specs=pl.BlockSpec((tm, tn), lambda i, j: (i, j)),
        out_shape=jax.ShapeDtypeStruct((m, n_j * tn), BF16),
        scratch_shapes=[pltpu.VMEM((tm, d), BF16)],
        compiler_params=_params(("arbitrary", "arbitrary")),
        name="in_proj",
    )(x2, g.reshape(1, d), sh, sc, w, cos, sa, sb)


def _rope_tables(seq):
    t = np.arange(seq)
    row = (t // GRID_W).astype(np.float32)
    col = (t % GRID_W).astype(np.float32)
    n_freq = DIFF_HEAD_DIM // 4
    inv = jnp.asarray(ROPE_BASE, F32) ** (-jnp.arange(n_freq, dtype=F32) / n_freq)
    ang_r = jnp.asarray(row)[:, None] * inv[None, :]
    ang_c = jnp.asarray(col)[:, None] * inv[None, :]
    ang = jnp.concatenate([ang_r, ang_r, ang_c, ang_c], axis=-1)
    cos, sin = jnp.cos(ang), jnp.sin(ang)
    first = (np.arange(DIFF_HEAD_DIM) % (2 * n_freq)) < n_freq
    sa = jnp.where(first[None, :], -sin, 0.0)
    sb = jnp.where(first[None, :], 0.0, sin)
    rep = LANES // DIFF_HEAD_DIM
    return tuple(jnp.tile(a, (1, rep)) for a in (cos, sa, sb))


def _na_row_start(r, rows):
    return np.clip(r - NA_WIN_ROWS // 2, 0, rows - NA_WIN_ROWS)


def _na_band_base(g, rows):
    return np.minimum(_na_row_start(g * NA_Q_ROWS, rows), rows - NA_BAND_ROWS)


def _na_variant_groups(rows):
    n_groups = rows // NA_Q_ROWS
    return (0, 1, n_groups - 1)


def _na_bias_table(rpb, rows):
    nq, nk = NA_Q_ROWS * GRID_W, NA_BAND_ROWS * GRID_W
    qi, kj = np.arange(nq), np.arange(nk)
    qc, kc = qi % GRID_W, kj % GRID_W
    col_start = np.clip(qc - NA_WIN_COLS // 2, 0, GRID_W - NA_WIN_COLS)
    col_ok = (kc[None, :] >= col_start[:, None]) & (kc[None, :] < col_start[:, None] + NA_WIN_COLS)
    cidx = np.clip(kc[None, :] - qc[:, None] + NA_WIN_COLS - 1, 0, 2 * NA_WIN_COLS - 2)
    tables = []
    for g in _na_variant_groups(rows):
        qr = g * NA_Q_ROWS + qi // GRID_W
        kr = _na_band_base(g, rows) + kj // GRID_W
        rs = _na_row_start(qr, rows)
        row_ok = (kr[None, :] >= rs[:, None]) & (kr[None, :] < rs[:, None] + NA_WIN_ROWS)
        ridx = np.clip(kr[None, :] - qr[:, None] + NA_WIN_ROWS - 1, 0, 2 * NA_WIN_ROWS - 2)
        bias = rpb[:, ridx, cidx].astype(F32)
        tables.append(jnp.where((row_ok & col_ok)[None], bias, MASK_VALUE))
    return jnp.stack(tables)


def _na_kernel(q_ref, k_ref, v_ref, kc_ref, vc_ref, bias_ref, o_ref, *, rows):
    g = pl.program_id(1)
    base = jnp.minimum(jnp.clip(g * NA_Q_ROWS - NA_WIN_ROWS // 2, 0, rows - NA_WIN_ROWS),
                       rows - NA_BAND_ROWS)
    start = pl.multiple_of(base * GRID_W, GRID_W)
    band = NA_BAND_ROWS * GRID_W
    scale = NA_HEAD_DIM ** -0.5
    for h in range(NA_HEADS):
        cs = slice(h * NA_HEAD_DIM, (h + 1) * NA_HEAD_DIM)
        q = q_ref[0, :, cs]
        kw = k_ref[0, pl.ds(start, band), cs]
        vw = v_ref[0, pl.ds(start, band), cs]
        s_w = _dot_nt(q, kw) * scale + bias_ref[0, h]
        s_c = _dot_nt(q, kc_ref[0, :, cs]) * scale
        m = jnp.maximum(jnp.max(s_w, axis=-1, keepdims=True), jnp.max(s_c, axis=-1, keepdims=True))
        p_w = jnp.exp(s_w - m)
        p_c = jnp.exp(s_c - m)
        l = jnp.sum(p_w, axis=-1, keepdims=True) + jnp.sum(p_c, axis=-1, keepdims=True)
        o = _dot(p_w.astype(BF16), vw) + _dot(p_c.astype(BF16), vc_ref[0, :, cs])
        o_ref[0, :, cs] = (o / l).astype(o_ref.dtype)


def _na_attention(proj, ctxp, bias, seq):
    b = proj.shape[0]
    lc = ctxp.shape[1]
    rows = seq // GRID_W
    n_groups = rows // NA_Q_ROWS
    nq, nk = NA_Q_ROWS * GRID_W, NA_BAND_ROWS * GRID_W

    def bias_map(bi, g):
        return (jnp.where(g == 0, 0, jnp.where(g == n_groups - 1, 2, 1)), 0, 0, 0)

    return pl.pallas_call(
        functools.partial(_na_kernel, rows=rows),
        grid=(b, n_groups),
        in_specs=[
            pl.BlockSpec((1, nq, NA_WIDTH), lambda bi, g: (bi, g, 0)),
            pl.BlockSpec((1, seq, NA_WIDTH), lambda bi, g: (bi, 0, 1)),
            pl.BlockSpec((1, seq, NA_WIDTH), lambda bi, g: (bi, 0, 2)),
            pl.BlockSpec((1, lc, NA_WIDTH), lambda bi, g: (bi, 0, 0)),
            pl.BlockSpec((1, lc, NA_WIDTH), lambda bi, g: (bi, 0, 1)),
            pl.BlockSpec((1, NA_HEADS, nq, nk), bias_map),
        ],
        out_specs=pl.BlockSpec((1, nq, NA_WIDTH), lambda bi, g: (bi, g, 0)),
        out_shape=jax.ShapeDtypeStruct((b, seq, NA_WIDTH), BF16),
        compiler_params=_params(("arbitrary", "arbitrary")),
        name="na_attn",
    )(proj, proj, proj, ctxp, ctxp, bias)


def _diff_kernel(q_ref, k_ref, v_ref, kc_ref, vc_ref, lam_ref, g_ref, o_ref):
    dl = lam_ref[...]
    lam = (jnp.exp(jnp.sum(dl[0:1] * dl[1:2], axis=-1, keepdims=True))
           - jnp.exp(jnp.sum(dl[2:3] * dl[3:4], axis=-1, keepdims=True)) + LAMBDA_INIT)
    q, k, kc = q_ref[0], k_ref[0], kc_ref[0]
    lane = lax.broadcasted_iota(jnp.int32, q.shape, 1)
    zero = jnp.zeros_like(q)

    def softmax_parts(qm):
        s = _dot_nt(qm, k)
        sc = _dot_nt(qm, kc)
        m = jnp.maximum(jnp.max(s, axis=-1, keepdims=True), jnp.max(sc, axis=-1, keepdims=True))
        e, ec = jnp.exp(s - m), jnp.exp(sc - m)
        inv = 1.0 / (jnp.sum(e, axis=-1, keepdims=True) + jnp.sum(ec, axis=-1, keepdims=True))
        return e, ec, inv

    e1, ec1, inv1 = softmax_parts(jnp.where(lane < DIFF_HEAD_DIM, q, zero))
    e2, ec2, inv2 = softmax_parts(jnp.where(lane >= DIFF_HEAD_DIM, q, zero))
    w2 = lam * inv2
    p = (e1 * inv1 - e2 * w2).astype(BF16)
    pc = (ec1 * inv1 - ec2 * w2).astype(BF16)
    o = _dot(p, v_ref[0]) + _dot(pc, vc_ref[0])
    o_ref[0] = (_rms(o, g_ref[...]) * (1.0 - LAMBDA_INIT)).astype(o_ref.dtype)


def _diff_attention(proj, ctxp, diff_lambda, diff_norm_g, seq, tq):
    b = proj.shape[0]
    lc = ctxp.shape[1]
    q0 = 3 * NA_WIDTH // LANES
    k0 = q0 + DIFF_QK_WIDTH // LANES
    v0 = k0 + DIFF_QK_WIDTH // LANES
    kc0 = 2 * NA_WIDTH // LANES
    vc0 = kc0 + DIFF_QK_WIDTH // LANES
    return pl.pallas_call(
        _diff_kernel,
        grid=(b, DIFF_HEADS, seq // tq),
        in_specs=[
            pl.BlockSpec((1, tq, LANES), lambda bi, h, i: (bi, i, q0 + h)),
            pl.BlockSpec((1, seq, LANES), lambda bi, h, i: (bi, 0, k0 + h)),
            pl.BlockSpec((1, seq, LANES), lambda bi, h, i: (bi, 0, v0 + h)),
            pl.BlockSpec((1, lc, LANES), lambda bi, h, i: (bi, 0, kc0 + h)),
            pl.BlockSpec((1, lc, LANES), lambda bi, h, i: (bi, 0, vc0 + h)),
            pl.BlockSpec((4, DIFF_HEAD_DIM), lambda bi, h, i: (0, 0)),
            pl.BlockSpec((1, DIFF_V_DIM), lambda bi, h, i: (0, 0)),
        ],
        out_specs=pl.BlockSpec((1, tq, LANES), lambda bi, h, i: (bi, i, h)),
        out_shape=jax.ShapeDtypeStruct((b, seq, DIFF_V_WIDTH), BF16),
        compiler_params=_params(("arbitrary", "arbitrary", "arbitrary")),
        name="diff_attn",
    )(proj, proj, proj, ctxp, ctxp, diff_lambda, diff_norm_g.reshape(1, DIFF_V_DIM))


def _merge_kernel(a_ref, b_ref, ga_ref, gb_ref, x_ref, g1_ref, sh_ref, sc_ref, ng_ref,
                  wa_ref, wb_ref, wo_ref, x1_ref, h2_ref):
    ya = _dot(a_ref[...], wa_ref[...])
    yb = _dot(b_ref[...], wb_ref[...])
    y = jax.nn.sigmoid(ga_ref[...].astype(F32)) * ya + jax.nn.sigmoid(gb_ref[...].astype(F32)) * yb
    x1 = x_ref[...] + g1_ref[0] * _dot(y.astype(BF16), wo_ref[...])
    x1_ref[...] = x1
    h2_ref[...] = (_rms(x1, ng_ref[...]) * (1.0 + sc_ref[0]) + sh_ref[0]).astype(BF16)


def _merge(a, bb, proj2, x2, g1, sh2, sc2, norm2_g, w_pa, w_pb, w_out, *, seq, tm):
    m, d = x2.shape
    ga0 = N_QKV // d
    mod_map = lambda i: ((i * tm) // seq, 0, 0)
    const = lambda i: (0, 0)
    resident = dict(pipeline_mode=pl.Buffered(1))
    return pl.pallas_call(
        _merge_kernel,
        grid=(m // tm,),
        in_specs=[
            pl.BlockSpec((tm, NA_WIDTH), lambda i: (i, 0)),
            pl.BlockSpec((tm, DIFF_V_WIDTH), lambda i: (i, 0)),
            pl.BlockSpec((tm, d), lambda i: (i, ga0)),
            pl.BlockSpec((tm, d), lambda i: (i, ga0 + 1)),
            pl.BlockSpec((tm, d), lambda i: (i, 0)),
            pl.BlockSpec((1, 1, d), mod_map),
            pl.BlockSpec((1, 1, d), mod_map),
            pl.BlockSpec((1, 1, d), mod_map),
            pl.BlockSpec((1, d), const),
            pl.BlockSpec((NA_WIDTH, d), const, **resident),
            pl.BlockSpec((DIFF_V_WIDTH, d), const, **resident),
            pl.BlockSpec((d, d), const, **resident),
        ],
        out_specs=[pl.BlockSpec((tm, d), lambda i: (i, 0)), pl.BlockSpec((tm, d), lambda i: (i, 0))],
        out_shape=[jax.ShapeDtypeStruct((m, d), F32), jax.ShapeDtypeStruct((m, d), BF16)],
        compiler_params=_params(("arbitrary",)),
        name="merge",
    )(a, bb, proj2, proj2, x2, g1, sh2, sc2, norm2_g.reshape(1, d), w_pa, w_pb, w_out)


def _ffn_up_kernel(h_ref, wg_ref, wu_ref, cwg_ref, cwu_ref, cbg_ref, cbu_ref, o_ref):
    h = h_ref[0]
    seq = h.shape[0]
    t = lax.broadcasted_iota(jnp.int32, (seq, 1), 0)
    first, last = t == 0, t == seq - 1

    def conv(u, cw, cb):
        prev = jnp.where(first, 0.0, pltpu.roll(u, 1, 0))
        nxt = jnp.where(last, 0.0, pltpu.roll(u, seq - 1, 0))
        return cw[0:1] * prev + cw[1:2] * u + cw[2:3] * nxt + cb

    gate = conv(_dot(h, wg_ref[...]), cwg_ref[...], cbg_ref[...])
    up = conv(_dot(h, wu_ref[...]), cwu_ref[...], cbu_ref[...])
    o_ref[0] = (gate * jax.nn.sigmoid(gate) * up).astype(o_ref.dtype)


def _ffn_up(h2, w_up, conv_w, conv_b, *, tn):
    b, seq, d = h2.shape
    n_j = D_FF // tn
    lo = lambda bi, j: (0, j)
    hi = lambda bi, j: (0, n_j + j)
    return pl.pallas_call(
        _ffn_up_kernel,
        grid=(b, n_j),
        in_specs=[
            pl.BlockSpec((1, seq, d), lambda bi, j: (bi, 0, 0)),
            pl.BlockSpec((d, tn), lo),
            pl.BlockSpec((d, tn), hi),
            pl.BlockSpec((3, tn), lo),
            pl.BlockSpec((3, tn), hi),
            pl.BlockSpec((1, tn), lo),
            pl.BlockSpec((1, tn), hi),
        ],
        out_specs=pl.BlockSpec((1, seq, tn), lambda bi, j: (bi, 0, j)),
        out_shape=jax.ShapeDtypeStruct((b, seq, D_FF), BF16),
        compiler_params=_params(("arbitrary", "arbitrary")),
        name="ffn_up",
    )(h2, w_up, w_up, conv_w, conv_w, conv_b.reshape(1, -1), conv_b.reshape(1, -1))


def _ffn_down_kernel(a_ref, w_ref, x_ref, g2_ref, fg_ref, o_ref, acc_ref):
    k = pl.program_id(1)

    @pl.when(k == 0)
    def _():
        acc_ref[...] = jnp.zeros_like(acc_ref)

    acc_ref[...] += _dot(a_ref[...], w_ref[...])

    @pl.when(k == pl.num_programs(1) - 1)
    def _():
        o_ref[...] = _rms(x_ref[...] + g2_ref[0] * acc_ref[...], fg_ref[...])


def _ffn_down(act2, w_down, x1, g2, final_g, *, seq, tm, tk):
    m, d = x1.shape
    mod_map = lambda i, k: ((i * tm) // seq, 0, 0)
    return pl.pallas_call(
        _ffn_down_kernel,
        grid=(m // tm, D_FF // tk),
        in_specs=[
            pl.BlockSpec((tm, tk), lambda i, k: (i, k)),
            pl.BlockSpec((tk, d), lambda i, k: (k, 0)),
            pl.BlockSpec((tm, d), lambda i, k: (i, 0)),
            pl.BlockSpec((1, 1, d), mod_map),
            pl.BlockSpec((1, d), lambda i, k: (0, 0)),
        ],
        out_specs=pl.BlockSpec((tm, d), lambda i, k: (i, 0)),
        out_shape=jax.ShapeDtypeStruct((m, d), F32),
        scratch_shapes=[pltpu.VMEM((tm, d), F32)],
        compiler_params=_params(("arbitrary", "arbitrary")),
        name="ffn_down",
    )(act2, w_down, x1, g2, final_g.reshape(1, d))


def kernel(x, c, ctx, c_ctx, ada_w, ada_b, norm1_g, w_in, na_rpb, diff_lambda, diff_norm_g,
           w_branch_a, w_branch_b, w_out, norm2_g, ffn_w_up, ffn_conv_w, ffn_conv_b, ffn_w_down,
           final_norm_g):
    assert ada_w.shape[0] == 1, "single-layer block"
    b, seq, d = x.shape
    lc = ctx.shape[1]
    m = b * seq

    mod_rows = -(-(b + 1) // 8) * 8
    cc = jnp.zeros((mod_rows, d), F32).at[:b].set(c).at[b].set(c_ctx)
    mod = _adaln(cc, ada_w[0], ada_b[0])
    sh1, sc1, g1, sh2, sc2, g2 = [mod[:, None, k * d:(k + 1) * d] for k in range(6)]

    w_in_b = w_in[0].astype(BF16)
    tables = _rope_tables(seq)
    tile = NA_WIDTH
    proj = _in_proj(x.reshape(m, d), norm1_g[0], sh1[:b], sc1[:b], w_in_b, tables,
                    rows_per_mod=seq, col_tiles=range((N_QKV + N_GATE) // tile),
                    rope_tiles=(3, 4), q_tile=3, tm=512, tn=tile)
    ctxp = _in_proj(ctx.reshape(b * lc, d), norm1_g[0], sh1[b:b + 1], sc1[b:b + 1], w_in_b, tables,
                    rows_per_mod=b * lc, col_tiles=(1, 2, 4, 5), rope_tiles=(), q_tile=-1,
                    tm=512, tn=tile)
    proj3 = proj.reshape(b, seq, -1)
    ctxp3 = ctxp.reshape(b, lc, -1)

    bias = _na_bias_table(na_rpb[0], seq // GRID_W)
    a_lat = _na_attention(proj3, ctxp3, bias, seq)
    b_lat = _diff_attention(proj3, ctxp3, diff_lambda[0], diff_norm_g[0], seq, tq=512)

    x1, h2 = _merge(a_lat.reshape(m, -1), b_lat.reshape(m, -1), proj, x.reshape(m, d),
                    g1[:b], sh2[:b], sc2[:b], norm2_g[0],
                    w_branch_a[0].astype(BF16), w_branch_b[0].astype(BF16), w_out[0].astype(BF16),
                    seq=seq, tm=256)

    act = _ffn_up(h2.reshape(b, seq, d), ffn_w_up[0].astype(BF16), ffn_conv_w[0], ffn_conv_b[0], tn=512)
    out = _ffn_down(act.reshape(m, D_FF), ffn_w_down[0].astype(BF16), x1, g2[:b], final_norm_g,
                    seq=seq, tm=512, tk=512)
    return out.reshape(b, seq, d)
```

```python
import functools
import math

import numpy as np
import jax
import jax.numpy as jnp
from jax import lax
from jax.experimental import pallas as pl
from jax.experimental.pallas import tpu as pltpu

D_MODEL = 2048
GRID_W = 64
NA_HEADS = 8
NA_HEAD_DIM = 128
NA_WIDTH = NA_HEADS * NA_HEAD_DIM
NA_WIN_ROWS = 8
NA_WIN_COLS = 16
DIFF_HEADS = 8
DIFF_HEAD_DIM = 64
DIFF_V_DIM = 2 * DIFF_HEAD_DIM
DIFF_QK_WIDTH = DIFF_HEADS * 2 * DIFF_HEAD_DIM
DIFF_V_WIDTH = DIFF_HEADS * DIFF_V_DIM
N_QKV = 3 * NA_WIDTH + 2 * DIFF_QK_WIDTH + DIFF_V_WIDTH
N_GATE = 2 * D_MODEL
D_FF = 5632
ROPE_BASE = 10000.0
EPS = 1e-6
LAMBDA_INIT = 0.8 - 0.6 * math.exp(-0.3 * 0)

LANES = 128
VMEM_LIMIT = 56 * 1024 * 1024
MASK_VALUE = -1e30

NA_Q_ROWS = 4
NA_BAND_ROWS = NA_Q_ROWS + NA_WIN_ROWS - 1

F32 = jnp.float32
BF16 = jnp.bfloat16


def _params(semantics):
    return pltpu.CompilerParams(dimension_semantics=semantics, vmem_limit_bytes=VMEM_LIMIT)


def _dot(a, b):
    return jnp.dot(a, b, preferred_element_type=F32)


def _dot_nt(a, b):
    return lax.dot_general(a, b, (((1,), (1,)), ((), ())), preferred_element_type=F32)


def _rms(x, g):
    return x * lax.rsqrt(jnp.mean(x * x, axis=-1, keepdims=True) + EPS) * g


def _adaln_kernel(c_ref, w_ref, b_ref, o_ref):
    c = c_ref[...]
    s = (c * jax.nn.sigmoid(c)).astype(BF16)
    o_ref[...] = _dot(s, w_ref[...].astype(BF16)) + b_ref[...]


def _adaln(cc, ada_w, ada_b):
    rows, d = cc.shape
    n = ada_w.shape[1]
    tn = 512
    return pl.pallas_call(
        _adaln_kernel,
        grid=(n // tn,),
        in_specs=[
            pl.BlockSpec((rows, d), lambda j: (0, 0)),
            pl.BlockSpec((d, tn), lambda j: (0, j)),
            pl.BlockSpec((1, tn), lambda j: (0, j)),
        ],
        out_specs=pl.BlockSpec((rows, tn), lambda j: (0, j)),
        out_shape=jax.ShapeDtypeStruct((rows, n), F32),
        compiler_params=_params(("arbitrary",)),
        name="adaln",
    )(cc, ada_w, ada_b.reshape(1, n))


def _in_proj_kernel(x_ref, g_ref, sh_ref, sc_ref, w_ref, cos_ref, sa_ref, sb_ref, o_ref, h_ref,
                    *, rope_tiles, q_tile):
    j = pl.program_id(1)

    @pl.when(j == 0)
    def _():
        h = _rms(x_ref[...], g_ref[...]) * (1.0 + sc_ref[0]) + sh_ref[0]
        h_ref[...] = h.astype(BF16)

    acc = _dot(h_ref[...], w_ref[...])
    is_rope = functools.reduce(jnp.logical_or, [j == t for t in rope_tiles]) if rope_tiles else None

    def plain():
        o_ref[...] = acc.astype(o_ref.dtype)

    def rope():
        qscale = jnp.where(j == q_tile, DIFF_HEAD_DIM ** -0.5, 1.0).astype(F32)
        cos, sa, sb = cos_ref[...], sa_ref[...], sb_ref[...]
        n = acc.shape[1]
        for hd in range(n // LANES):
            a = acc[:, hd * LANES:(hd + 1) * LANES]
            r = a * cos + pltpu.roll(a, LANES - 16, 1) * sa + pltpu.roll(a, 16, 1) * sb
            o_ref[:, hd * LANES:(hd + 1) * LANES] = (r * qscale).astype(o_ref.dtype)

    if rope_tiles:
        pl.when(is_rope)(rope)
        pl.when(jnp.logical_not(is_rope))(plain)
    else:
        plain()


def _in_proj(x2, g, sh, sc, w, tables, *, rows_per_mod, col_tiles, rope_tiles, q_tile, tm, tn):
    m, d = x2.shape
    cos, sa, sb = tables
    seq = cos.shape[0]
    n_j = len(col_tiles)
    col_tiles = tuple(col_tiles)

    def w_map(i, j):
        col = j
        if col_tiles != tuple(range(n_j)):
            col = sum(jnp.where(j == k, t, 0) for k, t in enumerate(col_tiles))
        return (0, col)

    mod_map = lambda i, j: ((i * tm) // rows_per_mod, 0, 0)
    tab_map = lambda i, j: ((i * tm % seq) // tm, 0)
    kern = functools.partial(_in_proj_kernel, rope_tiles=tuple(rope_tiles), q_tile=q_tile)
    return pl.pallas_call(
        kern,
        grid=(m // tm, n_j),
        in_specs=[
            pl.BlockSpec((tm, d), lambda i, j: (i, 0)),
            pl.BlockSpec((1, d), lambda i, j: (0, 0)),
            pl.BlockSpec((1, 1, d), mod_map),
            pl.BlockSpec((1, 1, d), mod_map),
            pl.BlockSpec((d, tn), w_map),
            pl.BlockSpec((tm, LANES), tab_map),
            pl.BlockSpec((tm, LANES), tab_map),
            pl.BlockSpec((tm, LANES), tab_map),
        ],
        out_specs=pl.BlockSpec((tm, tn), lambda i, j: (i, j)),
        out_shape=jax.ShapeDtypeStruct((m, n_j * tn), BF16),
        scratch_shapes=[pltpu.VMEM((tm, d), BF16)],
        compiler_params=_params(("arbitrary", "arbitrary")),
        name="in_proj",
    )(x2, g.reshape(1, d), sh, sc, w, cos, sa, sb)


def _rope_tables(seq):
    t = np.arange(seq)
    row = (t // GRID_W).astype(np.float32)
    col = (t % GRID_W).astype(np.float32)
    n_freq = DIFF_HEAD_DIM // 4
    inv = jnp.asarray(ROPE_BASE, F32) ** (-jnp.arange(n_freq, dtype=F32) / n_freq)
    ang_r = jnp.asarray(row)[:, None] * inv[None, :]
    ang_c = jnp.asarray(col)[:, None] * inv[None, :]
    ang = jnp.concatenate([ang_r, ang_r, ang_c, ang_c], axis=-1)
    cos, sin = jnp.cos(ang), jnp.sin(ang)
    first = (np.arange(DIFF_HEAD_DIM) % (2 * n_freq)) < n_freq
    sa = jnp.where(first[None, :], -sin, 0.0)
    sb = jnp.where(first[None, :], 0.0, sin)
    rep = LANES // DIFF_HEAD_DIM
    return tuple(jnp.tile(a, (1, rep)) for a in (cos, sa, sb))


def _na_row_start(r, rows):
    return np.clip(r - NA_WIN_ROWS // 2, 0, rows - NA_WIN_ROWS)


def _na_band_base(g, rows):
    return np.minimum(_na_row_start(g * NA_Q_ROWS, rows), rows - NA_BAND_ROWS)


def _na_variant_groups(rows):
    n_groups = rows // NA_Q_ROWS
    return (0, 1, n_groups - 1)


def _na_bias_table(rpb, rows):
    heads = rpb.shape[0]
    pad = GRID_W - 1
    padded = jnp.pad(rpb.astype(F32), ((0, 0), (0, 0), (pad, pad)))
    first = NA_WIN_COLS - 1 + pad
    toeplitz = jnp.stack([padded[:, :, first - qc:first - qc + GRID_W] for qc in range(GRID_W)], axis=2)
    qc, kc = np.arange(GRID_W)[:, None], np.arange(GRID_W)[None, :]
    col_start = np.clip(qc - NA_WIN_COLS // 2, 0, GRID_W - NA_WIN_COLS)
    col_ok = (kc >= col_start) & (kc < col_start + NA_WIN_COLS)
    toeplitz = jnp.where(col_ok[None, None], toeplitz, MASK_VALUE)
    masked = jnp.full((heads, GRID_W, GRID_W), MASK_VALUE, F32)
    tables = []
    for g in _na_variant_groups(rows):
        base = int(_na_band_base(g, rows))
        q_blocks = []
        for a in range(NA_Q_ROWS):
            r = g * NA_Q_ROWS + a
            rs = int(_na_row_start(r, rows))
            k_blocks = []
            for j in range(NA_BAND_ROWS):
                kr = base + j
                in_window = rs <= kr < rs + NA_WIN_ROWS
                k_blocks.append(toeplitz[:, kr - r + NA_WIN_ROWS - 1] if in_window else masked)
            q_blocks.append(jnp.concatenate(k_blocks, axis=-1))
        tables.append(jnp.concatenate(q_blocks, axis=-2))
    return jnp.stack(tables)


def _na_kernel(q_ref, k_ref, v_ref, kc_ref, vc_ref, bias_ref, o_ref, *, rows):
    g = pl.program_id(1)
    base = jnp.minimum(jnp.clip(g * NA_Q_ROWS - NA_WIN_ROWS // 2, 0, rows - NA_WIN_ROWS),
                       rows - NA_BAND_ROWS)
    start = pl.multiple_of(base * GRID_W, GRID_W)
    band = NA_BAND_ROWS * GRID_W
    scale = NA_HEAD_DIM ** -0.5
    for h in range(NA_HEADS):
        cs = slice(h * NA_HEAD_DIM, (h + 1) * NA_HEAD_DIM)
        q = q_ref[0, :, cs]
        kw = k_ref[0, pl.ds(start, band), cs]
        vw = v_ref[0, pl.ds(start, band), cs]
        s_w = _dot_nt(q, kw) * scale + bias_ref[0, h]
        s_c = _dot_nt(q, kc_ref[0, :, cs]) * scale
        m = jnp.maximum(jnp.max(s_w, axis=-1, keepdims=True), jnp.max(s_c, axis=-1, keepdims=True))
        p_w = jnp.exp(s_w - m)
        p_c = jnp.exp(s_c - m)
        l = jnp.sum(p_w, axis=-1, keepdims=True) + jnp.sum(p_c, axis=-1, keepdims=True)
        o = _dot(p_w.astype(BF16), vw) + _dot(p_c.astype(BF16), vc_ref[0, :, cs])
        o_ref[0, :, cs] = (o / l).astype(o_ref.dtype)


def _na_attention(proj, ctxp, bias, seq):
    b = proj.shape[0]
    lc = ctxp.shape[1]
    rows = seq // GRID_W
    n_groups = rows // NA_Q_ROWS
    nq, nk = NA_Q_ROWS * GRID_W, NA_BAND_ROWS * GRID_W

    def bias_map(bi, g):
        return (jnp.where(g == 0, 0, jnp.where(g == n_groups - 1, 2, 1)), 0, 0, 0)

    return pl.pallas_call(
        functools.partial(_na_kernel, rows=rows),
        grid=(b, n_groups),
        in_specs=[
            pl.BlockSpec((1, nq, NA_WIDTH), lambda bi, g: (bi, g, 0)),
            pl.BlockSpec((1, seq, NA_WIDTH), lambda bi, g: (bi, 0, 1)),
            pl.BlockSpec((1, seq, NA_WIDTH), lambda bi, g: (bi, 0, 2)),
            pl.BlockSpec((1, lc, NA_WIDTH), lambda bi, g: (bi, 0, 0)),
            pl.BlockSpec((1, lc, NA_WIDTH), lambda bi, g: (bi, 0, 1)),
            pl.BlockSpec((1, NA_HEADS, nq, nk), bias_map),
        ],
        out_specs=pl.BlockSpec((1, nq, NA_WIDTH), lambda bi, g: (bi, g, 0)),
        out_shape=jax.ShapeDtypeStruct((b, seq, NA_WIDTH), BF16),
        compiler_params=_params(("arbitrary", "arbitrary")),
        name="na_attn",
    )(proj, proj, proj, ctxp, ctxp, bias)


def _diff_kernel(q_ref, k_ref, v_ref, kc_ref, vc_ref, lam_ref, g_ref, o_ref, kall_ref, vall_ref, *, sub):
    dl = lam_ref[...]
    lam = (jnp.exp(jnp.sum(dl[0:1] * dl[1:2], axis=-1, keepdims=True))
           - jnp.exp(jnp.sum(dl[2:3] * dl[3:4], axis=-1, keepdims=True)) + LAMBDA_INIT)
    seq = k_ref.shape[1]

    @pl.when(pl.program_id(2) == 0)
    def _():
        kall_ref[:seq] = k_ref[0]
        kall_ref[seq:] = kc_ref[0]
        vall_ref[:seq] = v_ref[0]
        vall_ref[seq:] = vc_ref[0]

    tq = q_ref.shape[1]
    n_sub = tq // sub
    lane = lax.broadcasted_iota(jnp.int32, (sub, LANES), 1)
    zero = jnp.zeros((sub, LANES), BF16)

    def scores(t):
        q = q_ref[0, t * sub:(t + 1) * sub, :]
        kall = kall_ref[...]
        return (_dot_nt(jnp.where(lane < DIFF_HEAD_DIM, q, zero), kall),
                _dot_nt(jnp.where(lane >= DIFF_HEAD_DIM, q, zero), kall))

    def finish(t, s1, s2):
        e1 = jnp.exp(s1 - jnp.max(s1, axis=-1, keepdims=True))
        e2 = jnp.exp(s2 - jnp.max(s2, axis=-1, keepdims=True))
        l1 = jnp.sum(e1, axis=-1, keepdims=True)
        l2 = jnp.sum(e2, axis=-1, keepdims=True)
        p = (e1 - e2 * (lam * l1 / l2)).astype(BF16)
        o = _dot(p, vall_ref[...]) / l1
        o_ref[0, t * sub:(t + 1) * sub, :] = (_rms(o, g_ref[...]) * (1.0 - LAMBDA_INIT)).astype(o_ref.dtype)

    s = scores(0)
    for t in range(n_sub):
        s_next = scores(t + 1) if t + 1 < n_sub else None
        finish(t, *s)
        s = s_next


def _diff_attention(proj, ctxp, diff_lambda, diff_norm_g, seq, tq, sub):
    b = proj.shape[0]
    lc = ctxp.shape[1]
    q0 = 3 * NA_WIDTH // LANES
    k0 = q0 + DIFF_QK_WIDTH // LANES
    v0 = k0 + DIFF_QK_WIDTH // LANES
    kc0 = 2 * NA_WIDTH // LANES
    vc0 = kc0 + DIFF_QK_WIDTH // LANES
    return pl.pallas_call(
        functools.partial(_diff_kernel, sub=sub),
        scratch_shapes=[pltpu.VMEM((seq + lc, LANES), BF16), pltpu.VMEM((seq + lc, LANES), BF16)],
        grid=(b, DIFF_HEADS, seq // tq),
        in_specs=[
            pl.BlockSpec((1, tq, LANES), lambda bi, h, i: (bi, i, q0 + h)),
            pl.BlockSpec((1, seq, LANES), lambda bi, h, i: (bi, 0, k0 + h)),
            pl.BlockSpec((1, seq, LANES), lambda bi, h, i: (bi, 0, v0 + h)),
            pl.BlockSpec((1, lc, LANES), lambda bi, h, i: (bi, 0, kc0 + h)),
            pl.BlockSpec((1, lc, LANES), lambda bi, h, i: (bi, 0, vc0 + h)),
            pl.BlockSpec((4, DIFF_HEAD_DIM), lambda bi, h, i: (0, 0)),
            pl.BlockSpec((1, DIFF_V_DIM), lambda bi, h, i: (0, 0)),
        ],
        out_specs=pl.BlockSpec((1, tq, LANES), lambda bi, h, i: (bi, i, h)),
        out_shape=jax.ShapeDtypeStruct((b, seq, DIFF_V_WIDTH), BF16),
        compiler_params=_params(("arbitrary", "arbitrary", "arbitrary")),
        name="diff_attn",
    )(proj, proj, proj, ctxp, ctxp, diff_lambda, diff_norm_g.reshape(1, DIFF_V_DIM))


def _merge_kernel(a_ref, b_ref, ga_ref, gb_ref, x_ref, g1_ref, sh_ref, sc_ref, ng_ref,
                  wa_ref, wb_ref, wo_ref, x1_ref, h2_ref):
    ya = _dot(a_ref[...], wa_ref[...])
    yb = _dot(b_ref[...], wb_ref[...])
    y = jax.nn.sigmoid(ga_ref[...].astype(F32)) * ya + jax.nn.sigmoid(gb_ref[...].astype(F32)) * yb
    x1 = x_ref[...] + g1_ref[0] * _dot(y.astype(BF16), wo_ref[...])
    x1_ref[...] = x1
    h2_ref[...] = (_rms(x1, ng_ref[...]) * (1.0 + sc_ref[0]) + sh_ref[0]).astype(BF16)


def _merge(a, bb, proj2, x2, g1, sh2, sc2, norm2_g, w_pa, w_pb, w_out, *, seq, tm):
    m, d = x2.shape
    ga0 = N_QKV // d
    mod_map = lambda i: ((i * tm) // seq, 0, 0)
    const = lambda i: (0, 0)
    resident = dict(pipeline_mode=pl.Buffered(1))
    return pl.pallas_call(
        _merge_kernel,
        grid=(m // tm,),
        in_specs=[
            pl.BlockSpec((tm, NA_WIDTH), lambda i: (i, 0)),
            pl.BlockSpec((tm, DIFF_V_WIDTH), lambda i: (i, 0)),
            pl.BlockSpec((tm, d), lambda i: (i, ga0)),
            pl.BlockSpec((tm, d), lambda i: (i, ga0 + 1)),
            pl.BlockSpec((tm, d), lambda i: (i, 0)),
            pl.BlockSpec((1, 1, d), mod_map),
            pl.BlockSpec((1, 1, d), mod_map),
            pl.BlockSpec((1, 1, d), mod_map),
            pl.BlockSpec((1, d), const),
            pl.BlockSpec((NA_WIDTH, d), const, **resident),
            pl.BlockSpec((DIFF_V_WIDTH, d), const, **resident),
            pl.BlockSpec((d, d), const, **resident),
        ],
        out_specs=[pl.BlockSpec((tm, d), lambda i: (i, 0)), pl.BlockSpec((tm, d), lambda i: (i, 0))],
        out_shape=[jax.ShapeDtypeStruct((m, d), F32), jax.ShapeDtypeStruct((m, d), BF16)],
        compiler_params=_params(("arbitrary",)),
        name="merge",
    )(a, bb, proj2, proj2, x2, g1, sh2, sc2, norm2_g.reshape(1, d), w_pa, w_pb, w_out)


def _ffn_up_kernel(h_ref, wg_ref, wu_ref, cwg_ref, cwu_ref, cbg_ref, cbu_ref, o_ref):
    h = h_ref[0]
    seq = h.shape[0]
    t = lax.broadcasted_iota(jnp.int32, (seq, 1), 0)
    first, last = t == 0, t == seq - 1

    def conv(u, cw, cb):
        prev = jnp.where(first, 0.0, pltpu.roll(u, 1, 0))
        nxt = jnp.where(last, 0.0, pltpu.roll(u, seq - 1, 0))
        return cw[0:1] * prev + cw[1:2] * u + cw[2:3] * nxt + cb

    gate = conv(_dot(h, wg_ref[...]), cwg_ref[...], cbg_ref[...])
    up = conv(_dot(h, wu_ref[...]), cwu_ref[...], cbu_ref[...])
    o_ref[0] = (gate * jax.nn.sigmoid(gate) * up).astype(o_ref.dtype)


def _ffn_up(h2, w_up, conv_w, conv_b, *, tn):
    b, seq, d = h2.shape
    n_j = D_FF // tn
    lo = lambda bi, j: (0, j)
    hi = lambda bi, j: (0, n_j + j)
    return pl.pallas_call(
        _ffn_up_kernel,
        grid=(b, n_j),
        in_specs=[
            pl.BlockSpec((1, seq, d), lambda bi, j: (bi, 0, 0)),
            pl.BlockSpec((d, tn), lo),
            pl.BlockSpec((d, tn), hi),
            pl.BlockSpec((3, tn), lo),
            pl.BlockSpec((3, tn), hi),
            pl.BlockSpec((1, tn), lo),
            pl.BlockSpec((1, tn), hi),
        ],
        out_specs=pl.BlockSpec((1, seq, tn), lambda bi, j: (bi, 0, j)),
        out_shape=jax.ShapeDtypeStruct((b, seq, D_FF), BF16),
        compiler_params=_params(("arbitrary", "arbitrary")),
        name="ffn_up",
    )(h2, w_up, w_up, conv_w, conv_w, conv_b.reshape(1, -1), conv_b.reshape(1, -1))


def _ffn_down_kernel(a_ref, w_ref, x_ref, g2_ref, fg_ref, o_ref):
    o_ref[...] = _rms(x_ref[...] + g2_ref[0] * _dot(a_ref[...], w_ref[...]), fg_ref[...])


def _ffn_down(act2, w_down, x1, g2, final_g, *, seq, tm):
    m, d = x1.shape
    mod_map = lambda i: ((i * tm) // seq, 0, 0)
    return pl.pallas_call(
        _ffn_down_kernel,
        grid=(m // tm,),
        in_specs=[
            pl.BlockSpec((tm, D_FF), lambda i: (i, 0)),
            pl.BlockSpec((D_FF, d), lambda i: (0, 0), pipeline_mode=pl.Buffered(1)),
            pl.BlockSpec((tm, d), lambda i: (i, 0)),
            pl.BlockSpec((1, 1, d), mod_map),
            pl.BlockSpec((1, d), lambda i: (0, 0)),
        ],
        out_specs=pl.BlockSpec((tm, d), lambda i: (i, 0)),
        out_shape=jax.ShapeDtypeStruct((m, d), F32),
        compiler_params=_params(("arbitrary",)),
        name="ffn_down",
    )(act2, w_down, x1, g2, final_g.reshape(1, d))


def kernel(x, c, ctx, c_ctx, ada_w, ada_b, norm1_g, w_in, na_rpb, diff_lambda, diff_norm_g,
           w_branch_a, w_branch_b, w_out, norm2_g, ffn_w_up, ffn_conv_w, ffn_conv_b, ffn_w_down,
           final_norm_g):
    assert ada_w.shape[0] == 1, "single-layer block"
    b, seq, d = x.shape
    lc = ctx.shape[1]
    m = b * seq

    mod_rows = -(-(b + 1) // 8) * 8
    cc = jnp.zeros((mod_rows, d), F32).at[:b].set(c).at[b].set(c_ctx)
    mod = _adaln(cc, ada_w[0], ada_b[0])
    sh1, sc1, g1, sh2, sc2, g2 = [mod[:, None, k * d:(k + 1) * d] for k in range(6)]

    w_in_b = w_in[0].astype(BF16)
    tables = _rope_tables(seq)
    tile = NA_WIDTH
    proj = _in_proj(x.reshape(m, d), norm1_g[0], sh1[:b], sc1[:b], w_in_b, tables,
                    rows_per_mod=seq, col_tiles=range((N_QKV + N_GATE) // tile),
                    rope_tiles=(3, 4), q_tile=3, tm=1024, tn=tile)
    ctxp = _in_proj(ctx.reshape(b * lc, d), norm1_g[0], sh1[b:b + 1], sc1[b:b + 1], w_in_b, tables,
                    rows_per_mod=b * lc, col_tiles=(1, 2, 4, 5), rope_tiles=(), q_tile=-1,
                    tm=512, tn=tile)
    proj3 = proj.reshape(b, seq, -1)
    ctxp3 = ctxp.reshape(b, lc, -1)

    bias = _na_bias_table(na_rpb[0], seq // GRID_W)
    a_lat = _na_attention(proj3, ctxp3, bias, seq)
    b_lat = _diff_attention(proj3, ctxp3, diff_lambda[0], diff_norm_g[0], seq, tq=512, sub=128)

    x1, h2 = _merge(a_lat.reshape(m, -1), b_lat.reshape(m, -1), proj, x.reshape(m, d),
                    g1[:b], sh2[:b], sc2[:b], norm2_g[0],
                    w_branch_a[0].astype(BF16), w_branch_b[0].astype(BF16), w_out[0].astype(BF16),
                    seq=seq, tm=256)

    act = _ffn_up(h2.reshape(b, seq, d), ffn_w_up[0].astype(BF16), ffn_conv_w[0], ffn_conv_b[0], tn=512)
    out = _ffn_down(act.reshape(m, D_FF), ffn_w_down[0].astype(BF16), x1, g2[:b], final_norm_g,
                    seq=seq, tm=256)
    return out.reshape(b, seq, d)
```
